```python
import math
import jax, jax.numpy as jnp
from jax import lax
import numpy as np

D_MODEL = 2048
BATCH = 8
SEQ = 2048
DEPTH = 2

CHUNK = 64
PLE_DIM = 256
NORM_EPS = 1e-6

SSD_HEADS = 16
SSD_HEAD_DIM = 64
SSD_WIDTH = SSD_HEADS * SSD_HEAD_DIM
SSD_GROUPS = 2
SSD_STATE = 128
SSD_CONV = 4
SSD_CONV_CH = SSD_WIDTH + 2 * SSD_GROUPS * SSD_STATE

GLA_HEADS = 4
GLA_DK = 128
GLA_DV = 256
GLA_WIDTH = GLA_HEADS * GLA_DV
GLA_GATE_RANK = 16
GLA_GATE_NORMALIZER = 16.0

MIX_WIDTH = SSD_WIDTH + GLA_WIDTH

D_FF = 5632
FFN_CONV = 3

IN_SIZES = (SSD_WIDTH, SSD_WIDTH, SSD_GROUPS * SSD_STATE, SSD_GROUPS * SSD_STATE, SSD_HEADS,
            GLA_HEADS * GLA_DK, GLA_HEADS * GLA_DK, GLA_WIDTH, GLA_WIDTH, GLA_GATE_RANK)
IN_COLS = sum(IN_SIZES)

kernel_name = "hymba_ssd_gla_convffn_ple"


def _split_points(sizes):
    pts, acc = [], 0
    for s in sizes[:-1]:
        acc += s
        pts.append(acc)
    return pts


def _rmsnorm(x, g):
    xf = x.astype(jnp.float32)
    y = xf * lax.rsqrt(jnp.mean(xf * xf, axis=-1, keepdims=True) + NORM_EPS)
    return (y * g.astype(jnp.float32)).astype(x.dtype)


def _causal_dwconv(u, w, b):
    k, s = w.shape[0], u.shape[1]
    up = jnp.pad(u, ((0, 0), (k - 1, 0), (0, 0)))
    out = b
    for j in range(k):
        out = out + up[:, j:j + s] * w[j]
    return out


def _chunk_scan(decay, inc):
    def step(h, xs):
        d, u = xs
        return d * h + u, h
    h0 = jnp.zeros_like(inc[:, 0])
    _, h_prev = lax.scan(step, h0, (jnp.moveaxis(decay, 1, 0), jnp.moveaxis(inc, 1, 0)))
    return jnp.moveaxis(h_prev, 0, 1)


def _ssd_mixer(xs, z, bm, cm, dt_raw, conv_w, conv_b, dt_bias, a_log, d_skip, norm_g):
    f32 = jnp.float32
    b, s, _ = xs.shape
    nc = s // CHUNK
    r = SSD_HEADS // SSD_GROUPS
    xbc = jax.nn.silu(_causal_dwconv(jnp.concatenate([xs, bm, cm], axis=-1), conv_w, conv_b))
    xs, bm, cm = jnp.split(xbc, [SSD_WIDTH, SSD_WIDTH + SSD_GROUPS * SSD_STATE], axis=-1)
    x = xs.astype(f32).reshape(b, nc, CHUNK, SSD_GROUPS, r, SSD_HEAD_DIM)
    bc = bm.astype(f32).reshape(b, nc, CHUNK, SSD_GROUPS, SSD_STATE)
    cc = cm.astype(f32).reshape(b, nc, CHUNK, SSD_GROUPS, SSD_STATE)
    dt = jax.nn.softplus(dt_raw.astype(f32) + dt_bias.astype(f32)).reshape(b, nc, CHUNK, SSD_GROUPS, r)
    a = dt * (-jnp.exp(a_log.astype(f32))).reshape(SSD_GROUPS, r)
    a_cs = jnp.cumsum(a, axis=2)
    xdt = x * dt[..., None]
    a_t = jnp.moveaxis(a_cs, 2, -1)
    mask = jnp.tril(jnp.ones((CHUNK, CHUNK), dtype=bool))
    decay = jnp.exp(jnp.where(mask, a_t[..., :, None] - a_t[..., None, :], -jnp.inf))
    cb = jnp.einsum('bclgn,bcsgn->bcgls', cc, bc)
    y_diag = jnp.einsum('bcgls,bcgrls,bcsgrp->bclgrp', cb, decay, xdt)
    states = jnp.einsum('bcsgn,bcsgr,bcsgrp->bcgrpn', bc, jnp.exp(a_cs[:, :, -1:] - a_cs), xdt)
    h_prev = _chunk_scan(jnp.exp(a_cs[:, :, -1])[..., None, None], states)
    y_off = jnp.einsum('bclgn,bcgrpn,bclgr->bclgrp', cc, h_prev, jnp.exp(a_cs))
    y = y_diag + y_off + x * d_skip.astype(f32).reshape(SSD_GROUPS, r, 1)
    y = y.reshape(b, s, SSD_WIDTH) * jax.nn.silu(z.astype(f32))
    y = _rmsnorm(y.reshape(b, s, SSD_GROUPS, SSD_WIDTH // SSD_GROUPS),
                 norm_g.reshape(SSD_GROUPS, SSD_WIDTH // SSD_GROUPS)).reshape(b, s, SSD_WIDTH)
    return y.astype(z.dtype)


def _gla_mixer(q, k, v, g_out, gk_low, gk_up, gk_bias, norm_g):
    f32 = jnp.float32
    b, s, _ = q.shape
    nc = s // CHUNK
    q = q.astype(f32).reshape(b, nc, CHUNK, GLA_HEADS, GLA_DK) * (GLA_DK ** -0.5)
    k = k.astype(f32).reshape(b, nc, CHUNK, GLA_HEADS, GLA_DK)
    v = v.astype(f32).reshape(b, nc, CHUNK, GLA_HEADS, GLA_DV)
    gk = jax.nn.log_sigmoid((gk_low @ gk_up + gk_bias).astype(f32)) / GLA_GATE_NORMALIZER
    bcum = jnp.cumsum(gk.reshape(b, nc, CHUNK, GLA_HEADS, GLA_DK), axis=2)
    q_e = q * jnp.exp(bcum)
    k_e = k * jnp.exp(-bcum)
    mask = jnp.tril(jnp.ones((CHUNK, CHUNK), dtype=bool))
    attn = jnp.where(mask, jnp.einsum('bclhd,bcshd->bchls', q_e, k_e), 0.0)
    o = jnp.einsum('bchls,bcshv->bclhv', attn, v)
    kv = jnp.einsum('bcshd,bcshv->bchdv', k * jnp.exp(bcum[:, :, -1:] - bcum), v)
    s_prev = _chunk_scan(jnp.exp(bcum[:, :, -1])[..., None], kv)
    o = o + jnp.einsum('bclhd,bchdv->bclhv', q_e, s_prev)
    o = _rmsnorm(o, norm_g)
    o = o.reshape(b, s, GLA_WIDTH) * jax.nn.silu(g_out.astype(f32))
    return o.astype(g_out.dtype)


def _normal(k, shape, scale):
    return jax.random.normal(k, shape, jnp.float32) * scale


def setup_inputs(seed: int = 0) -> dict:
    key = jax.random.key(seed)
    ks = jax.random.split(key, 24)
    dt = jnp.exp(jax.random.uniform(ks[6], (DEPTH, SSD_HEADS), jnp.float32)
                 * (math.log(0.1) - math.log(0.001)) + math.log(0.001))
    return {
        "x": _normal(ks[0], (BATCH, SEQ, D_MODEL), 1.0),
        "p": _normal(ks[1], (DEPTH, BATCH, SEQ, PLE_DIM), 1.0),
        "norm_mix": 1.0 + _normal(ks[2], (DEPTH, D_MODEL), 0.02),
        "w_in": _normal(ks[3], (DEPTH, D_MODEL, IN_COLS), D_MODEL ** -0.5),
        "ssd_conv_w": _normal(ks[4], (DEPTH, SSD_CONV, SSD_CONV_CH), SSD_CONV ** -0.5),
        "ssd_conv_b": _normal(ks[5], (DEPTH, SSD_CONV_CH), 0.02),
        "ssd_dt_bias": dt + jnp.log(-jnp.expm1(-dt)),
        "ssd_a_log": jnp.log(jax.random.uniform(ks[7], (DEPTH, SSD_HEADS), jnp.float32, 1.0, 16.0)),
        "ssd_d": 1.0 + _normal(ks[8], (DEPTH, SSD_HEADS), 0.02),
        "ssd_norm": 1.0 + _normal(ks[9], (DEPTH, SSD_WIDTH), 0.02),
        "gla_gk_up": _normal(ks[10], (DEPTH, GLA_GATE_RANK, GLA_HEADS * GLA_DK), GLA_GATE_RANK ** -0.5),
        "gla_gk_bias": _normal(ks[11], (DEPTH, GLA_HEADS * GLA_DK), 0.02),
        "gla_norm": 1.0 + _normal(ks[12], (DEPTH, GLA_DV), 0.02),
        "w_out": _normal(ks[13], (DEPTH, MIX_WIDTH, D_MODEL), MIX_WIDTH ** -0.5),
        "norm_ffn": 1.0 + _normal(ks[14], (DEPTH, D_MODEL), 0.02),
        "ffn_w_up": _normal(ks[15], (DEPTH, D_MODEL, 2 * D_FF), D_MODEL ** -0.5),
        "ffn_conv_w": _normal(ks[16], (DEPTH, FFN_CONV, 2 * D_FF), FFN_CONV ** -0.5),
        "ffn_conv_b": _normal(ks[17], (DEPTH, 2 * D_FF), 0.02),
        "ffn_w_down": _normal(ks[18], (DEPTH, D_FF, D_MODEL), D_FF ** -0.5),
        "norm_ple": 1.0 + _normal(ks[19], (DEPTH, D_MODEL), 0.02),
        "ple_w_gate": _normal(ks[20], (DEPTH, D_MODEL, D_MODEL), D_MODEL ** -0.5),
        "ple_w_proj": _normal(ks[21], (DEPTH, PLE_DIM, D_MODEL), PLE_DIM ** -0.5),
        "norm_final": 1.0 + _normal(ks[22], (D_MODEL,), 0.02),
    }


def reference(x, p, norm_mix, w_in, ssd_conv_w, ssd_conv_b, ssd_dt_bias, ssd_a_log, ssd_d,
              ssd_norm, gla_gk_up, gla_gk_bias, gla_norm, w_out, norm_ffn, ffn_w_up,
              ffn_conv_w, ffn_conv_b, ffn_w_down, norm_ple, ple_w_gate, ple_w_proj, norm_final):
    split_pts = _split_points(IN_SIZES)
    for i in range(DEPTH):
        h = _rmsnorm(x, norm_mix[i])
        proj = h @ w_in[i]
        (s_x, s_z, s_b, s_c, s_dt, g_q, g_k, g_v, g_g, g_gk) = jnp.split(proj, split_pts, axis=-1)
        y_ssd = _ssd_mixer(s_x, s_z, s_b, s_c, s_dt, ssd_conv_w[i], ssd_conv_b[i],
                           ssd_dt_bias[i], ssd_a_log[i], ssd_d[i], ssd_norm[i])
        y_gla = _gla_mixer(g_q, g_k, g_v, g_g, g_gk, gla_gk_up[i], gla_gk_bias[i], gla_norm[i])
        mix = jnp.concatenate([y_ssd, y_gla], axis=-1) @ w_out[i]
        x = x + mix.astype(x.dtype)
        h = _rmsnorm(x, norm_ffn[i])
        u = _causal_dwconv(h @ ffn_w_up[i], ffn_conv_w[i], ffn_conv_b[i])
        gate, val = jnp.split(u, [D_FF], axis=-1)
        x = x + ((jax.nn.silu(gate) * val) @ ffn_w_down[i]).astype(x.dtype)
        pg = jax.nn.sigmoid(_rmsnorm(x, norm_ple[i]) @ ple_w_gate[i])
        x = x + (pg * (p[i] @ ple_w_proj[i])).astype(x.dtype)
    return _rmsnorm(x, norm_final)
```

```python
import functools

import jax
import jax.numpy as jnp
from jax import lax
from jax.experimental import pallas as pl
from jax.experimental.pallas import tpu as pltpu

F32 = jnp.float32
BF16 = jnp.bfloat16

NORM_EPS = 1e-6
D_MODEL = 2048
PLE_DIM = 256

SSD_HEADS = 16
SSD_HEAD_DIM = 64
SSD_WIDTH = SSD_HEADS * SSD_HEAD_DIM
SSD_GROUPS = 2
SSD_STATE = 128
SSD_CONV = 4
SSD_BC = SSD_GROUPS * SSD_STATE
SSD_CONV_CH = SSD_WIDTH + 2 * SSD_BC
SSD_GROUP_WIDTH = SSD_WIDTH // SSD_GROUPS
HEADS_PER_GROUP = SSD_HEADS // SSD_GROUPS

GLA_HEADS = 4
GLA_DK = 128
GLA_DV = 256
GLA_QK = GLA_HEADS * GLA_DK
GLA_WIDTH = GLA_HEADS * GLA_DV
GLA_GATE_RANK = 16
GLA_GATE_NORMALIZER = 16.0
GLA_CHUNK = 64

MIX_WIDTH = SSD_WIDTH + GLA_WIDTH
D_FF = 5632
FFN_CONV = 3

LANES = 128
SUBLANES = 8

C_XBC = 0
C_Z = C_XBC + SSD_CONV_CH
C_Q = C_Z + SSD_WIDTH
C_K = C_Q + GLA_QK
C_V = C_K + GLA_QK
C_G = C_V + GLA_WIDTH
C_SMALL = C_G + GLA_WIDTH
PROJ_COLS = C_SMALL + LANES

MIX_BLOCK = 128

VMEM_LIMIT = 56 * 1024 * 1024


def _dot(a, b):
    return jnp.dot(a, b, preferred_element_type=F32)


def _dot_nt(a, b):
    return lax.dot_general(a, b, (((1,), (1,)), ((), ())), preferred_element_type=F32)


def _dot_tn(a, b):
    return lax.dot_general(a, b, (((0,), (0,)), ((), ())), preferred_element_type=F32)


def _split3(a):
    hi = a.astype(BF16)
    r = a - hi.astype(F32)
    mid = r.astype(BF16)
    lo = (r - mid.astype(F32)).astype(BF16)
    return hi, mid, lo


def _dot01_left(m01, a):
    hi, mid, lo = _split3(a)
    return _dot(m01, hi) + _dot(m01, mid) + _dot(m01, lo)


def _dot01_right(a, m01):
    hi, mid, lo = _split3(a)
    return _dot(hi, m01) + _dot(mid, m01) + _dot(lo, m01)


def _silu(x):
    return x * jax.nn.sigmoid(x)


def _softplus(x):
    return jnp.maximum(x, 0.0) + jnp.log1p(jnp.exp(-jnp.abs(x)))


def _rms_scale(x):
    return lax.rsqrt(jnp.mean(x * x, axis=-1, keepdims=True) + NORM_EPS)


def _norm_matmul_kernel(x_ref, g_ref, w_ref, o_ref, h_ref):
    @pl.when(pl.program_id(1) == 0)
    def _():
        x = x_ref[...]
        h_ref[...] = (x * _rms_scale(x) * g_ref[...]).astype(BF16)

    o_ref[...] = _dot(h_ref[...], w_ref[...])


def _norm_matmul(x, g, w, *, tm, tn):
    t, d = x.shape
    n = w.shape[1]
    return pl.pallas_call(
        _norm_matmul_kernel,
        grid=(t // tm, n // tn),
        in_specs=[
            pl.BlockSpec((tm, d), lambda i, j: (i, 0)),
            pl.BlockSpec((1, d), lambda i, j: (0, 0)),
            pl.BlockSpec((d, tn), lambda i, j: (0, j)),
        ],
        out_specs=pl.BlockSpec((tm, tn), lambda i, j: (i, j)),
        out_shape=jax.ShapeDtypeStruct((t, n), F32),
        scratch_shapes=[pltpu.VMEM((tm, d), BF16)],
        compiler_params=pltpu.CompilerParams(
            dimension_semantics=("arbitrary", "arbitrary"), vmem_limit_bytes=VMEM_LIMIT),
        name="norm_in_proj",
    )(x, g, w)


def _mixer_kernel(proj_ref, convw_ref, convb_ref, dtb_ref, alog_ref, dskip_ref, ssdn_ref,
                  gkup_hi_ref, gkup_lo_ref, gkb_ref, glan_ref,
                  tri_ref, tri64_ref, ones64_ref, expand_ref,
                  y_ref,
                  ext_ref, ssd_state_ref, gla_state_ref):
    L = MIX_BLOCK
    j = pl.program_id(1)

    @pl.when(j == 0)
    def _():
        ext_ref[0:SUBLANES, :] = jnp.zeros((SUBLANES, SSD_CONV_CH), F32)
        ssd_state_ref[...] = jnp.zeros_like(ssd_state_ref)
        gla_state_ref[...] = jnp.zeros_like(gla_state_ref)

    ext_ref[SUBLANES:SUBLANES + L, :] = proj_ref[:, C_XBC:C_XBC + SSD_CONV_CH]
    conv = convb_ref[...]
    for tap in range(SSD_CONV):
        off = SUBLANES - (SSD_CONV - 1 - tap)
        conv = conv + convw_ref[tap:tap + 1, :] * ext_ref[off:off + L, :]
    ext_ref[0:SUBLANES, :] = ext_ref[L:L + SUBLANES, :]
    xbc = _silu(conv)
    xs = xbc[:, 0:SSD_WIDTH]
    bm = xbc[:, SSD_WIDTH:SSD_WIDTH + SSD_BC].astype(BF16)
    cm = xbc[:, SSD_WIDTH + SSD_BC:SSD_CONV_CH].astype(BF16)

    small = proj_ref[:, C_SMALL:C_SMALL + LANES]

    dt = _softplus(small + dtb_ref[...])
    a = dt * (-jnp.exp(alog_ref[...]))
    acs = _dot01_left(tri_ref[...], a)
    a_last = acs[L - 1:L, :]
    expand = expand_ref[...]
    dt_x = _dot01_right(dt, expand)
    e_x = _dot01_right(jnp.exp(acs), expand)
    w_x = _dot01_right(jnp.exp(a_last - acs), expand)
    xdt = xs * dt_x
    xdt_b = xdt.astype(BF16)
    lane = lax.broadcasted_iota(jnp.int32, (L, SSD_WIDTH), 1)
    first_of_pair = (lane & SSD_HEAD_DIM) == 0
    zero_b = jnp.zeros_like(xdt_b)
    xdt_even = jnp.where(first_of_pair, xdt_b, zero_b)
    xdt_odd = jnp.where(first_of_pair, zero_b, xdt_b)
    xdtw_b = (xdt * w_x).astype(BF16)

    acs_t = acs.T
    row = lax.broadcasted_iota(jnp.int32, (L, L), 0)
    col = lax.broadcasted_iota(jnp.int32, (L, L), 1)
    causal = row >= col

    for g in range(SSD_GROUPS):
        gs = slice(g * SSD_STATE, (g + 1) * SSD_STATE)
        gw = slice(g * SSD_GROUP_WIDTH, (g + 1) * SSD_GROUP_WIDTH)
        cm_g = cm[:, gs]
        bm_g = bm[:, gs]
        cb = _dot_nt(cm_g, bm_g)
        pair_out = []
        for pair in range(HEADS_PER_GROUP // 2):
            acc = None
            for sub, rhs_all in ((0, xdt_even), (1, xdt_odd)):
                h = g * HEADS_PER_GROUP + 2 * pair + sub
                diff = acs[:, h:h + 1] - acs_t[h:h + 1, :]
                m = (jnp.where(causal, jnp.exp(diff), 0.0) * cb).astype(BF16)
                c0 = (h // 2) * LANES
                part = _dot(m, rhs_all[:, c0:c0 + LANES])
                acc = part if acc is None else acc + part
            pair_out.append(acc)
        s_prev = ssd_state_ref[g]
        y_off = _dot(cm_g, s_prev.astype(BF16)) * e_x[:, gw]
        new_state = _dot_tn(bm_g, xdtw_b[:, gw])
        ssd_state_ref[g] = e_x[L - 1:L, gw] * s_prev + new_state
        y_g = jnp.concatenate(pair_out, axis=-1) + y_off
        y_g = y_g + xs[:, gw] * dskip_ref[:, gw]
        y_g = y_g * _silu(proj_ref[:, C_Z + g * SSD_GROUP_WIDTH:C_Z + (g + 1) * SSD_GROUP_WIDTH])
        y_g = y_g * _rms_scale(y_g) * ssdn_ref[:, gw]
        y_ref[:, gw] = y_g.astype(y_ref.dtype)

    s_hi = small.astype(BF16)
    s_lo = (small - s_hi.astype(F32)).astype(BF16)
    gkup_hi = gkup_hi_ref[...]
    pre = (_dot(s_hi, gkup_hi) + _dot(s_hi, gkup_lo_ref[...]) + _dot(s_lo, gkup_hi)
           + gkb_ref[...])
    gk = (jnp.minimum(pre, 0.0) - jnp.log1p(jnp.exp(-jnp.abs(pre)))) * (1.0 / GLA_GATE_NORMALIZER)
    bcum = _dot01_left(tri64_ref[...], gk)
    btot = _dot01_left(ones64_ref[...], gk)
    q = proj_ref[:, C_Q:C_Q + GLA_QK] * (GLA_DK ** -0.5)
    k = proj_ref[:, C_K:C_K + GLA_QK]
    q_e = (q * jnp.exp(bcum)).astype(BF16)
    k_e = (k * jnp.exp(-bcum)).astype(BF16)
    k_d = (k * jnp.exp(btot - bcum)).astype(BF16)
    chunk_end_decay = jnp.exp(btot)
    v = proj_ref[:, C_V:C_V + GLA_WIDTH].astype(BF16)
    causal64 = causal & ((row // GLA_CHUNK) == (col // GLA_CHUNK))

    for h in range(GLA_HEADS):
        ks = slice(h * GLA_DK, (h + 1) * GLA_DK)
        vs = slice(h * GLA_DV, (h + 1) * GLA_DV)
        attn = jnp.where(causal64, _dot_nt(q_e[:, ks], k_e[:, ks]), 0.0).astype(BF16)
        o = _dot(attn, v[:, vs])
        state_t = gla_state_ref[h]
        inter = []
        for c in range(L // GLA_CHUNK):
            rs = slice(c * GLA_CHUNK, (c + 1) * GLA_CHUNK)
            inter.append(_dot_nt(q_e[rs, ks], state_t.astype(BF16)))
            kv_t = _dot_tn(v[rs, vs], k_d[rs, ks])
            state_t = chunk_end_decay[c * GLA_CHUNK:c * GLA_CHUNK + 1, ks] * state_t + kv_t
        gla_state_ref[h] = state_t
        o = o + jnp.concatenate(inter, axis=0)
        o = o * _rms_scale(o) * glan_ref[...]
        o = o * _silu(proj_ref[:, C_G + h * GLA_DV:C_G + (h + 1) * GLA_DV])
        y_ref[:, SSD_WIDTH + h * GLA_DV:SSD_WIDTH + (h + 1) * GLA_DV] = o.astype(y_ref.dtype)


def _mixer(proj, consts, params, *, batch, seq):
    L = MIX_BLOCK
    nblk = seq // L
    full = lambda a: pl.BlockSpec(a.shape, lambda b, j: (0,) * a.ndim)
    operands = list(params) + list(consts)
    return pl.pallas_call(
        _mixer_kernel,
        grid=(batch, nblk),
        in_specs=[pl.BlockSpec((L, PROJ_COLS), lambda b, j: (b * nblk + j, 0))]
                 + [full(a) for a in operands],
        out_specs=pl.BlockSpec((L, MIX_WIDTH), lambda b, j: (b * nblk + j, 0)),
        out_shape=jax.ShapeDtypeStruct((batch * seq, MIX_WIDTH), BF16),
        scratch_shapes=[
            pltpu.VMEM((L + SUBLANES, SSD_CONV_CH), F32),
            pltpu.VMEM((SSD_GROUPS, SSD_STATE, SSD_GROUP_WIDTH), F32),
            pltpu.VMEM((GLA_HEADS, GLA_DV, GLA_DK), F32),
        ],
        compiler_params=pltpu.CompilerParams(
            dimension_semantics=("arbitrary", "arbitrary"), vmem_limit_bytes=VMEM_LIMIT),
        name="token_mixer",
    )(proj, *operands)


def _matmul_residual_kernel(y_ref, w_ref, x_ref, o_ref):
    o_ref[...] = x_ref[...] + _dot(y_ref[...], w_ref[...])


def _matmul_residual(y, w, x, *, tm, tn):
    t, k = y.shape
    n = w.shape[1]
    return pl.pallas_call(
        _matmul_residual_kernel,
        grid=(t // tm, n // tn),
        in_specs=[
            pl.BlockSpec((tm, k), lambda i, j: (i, 0)),
            pl.BlockSpec((k, tn), lambda i, j: (0, j)),
            pl.BlockSpec((tm, tn), lambda i, j: (i, j)),
        ],
        out_specs=pl.BlockSpec((tm, tn), lambda i, j: (i, j)),
        out_shape=jax.ShapeDtypeStruct((t, n), F32),
        compiler_params=pltpu.CompilerParams(
            dimension_semantics=("arbitrary", "arbitrary"), vmem_limit_bytes=VMEM_LIMIT),
        name="out_proj_residual",
    )(y, w, x)


def _ffn_kernel(x_ref, g_ref, wg_ref, wv_ref, cwg_ref, cwv_ref, cbg_ref, cbv_ref, wd_ref,
                o_ref, h_ref, ext_ref, carry_ref, *, blocks_per_seq):
    tm = x_ref.shape[0]
    i = pl.program_id(0)
    f = pl.program_id(1)
    seq_start = (i % blocks_per_seq) == 0

    @pl.when(f == 0)
    def _():
        x = x_ref[...]
        h_ref[...] = (x * _rms_scale(x) * g_ref[...]).astype(BF16)

    def conv_branch(w_ref, cw_ref, cb_ref, which):
        u = _dot(h_ref[...], w_ref[...])
        ext_ref[SUBLANES:SUBLANES + tm, :] = u

        @pl.when(seq_start)
        def _():
            ext_ref[0:SUBLANES, :] = jnp.zeros((SUBLANES, u.shape[1]), F32)

        @pl.when(jnp.logical_not(seq_start))
        def _():
            ext_ref[0:SUBLANES, :] = carry_ref[which, f]

        out = cb_ref[...] + cw_ref[FFN_CONV - 1:FFN_CONV, :] * u
        for tap in range(FFN_CONV - 1):
            off = SUBLANES - (FFN_CONV - 1 - tap)
            out = out + cw_ref[tap:tap + 1, :] * ext_ref[off:off + tm, :]
        carry_ref[which, f] = ext_ref[tm:tm + SUBLANES, :]
        return out

    gate = conv_branch(wg_ref, cwg_ref, cbg_ref, 0)
    val = conv_branch(wv_ref, cwv_ref, cbv_ref, 1)
    act = (_silu(gate) * val).astype(BF16)
    contrib = _dot(act, wd_ref[...])

    @pl.when(f == 0)
    def _():
        o_ref[...] = x_ref[...] + contrib

    @pl.when(f > 0)
    def _():
        o_ref[...] += contrib


def _ffn(x, g, w_up, conv_w, conv_b, w_down, *, tm, tf, seq):
    t, d = x.shape
    nf = D_FF // tf
    kern = functools.partial(_ffn_kernel, blocks_per_seq=seq // tm)
    return pl.pallas_call(
        kern,
        grid=(t // tm, nf),
        in_specs=[
            pl.BlockSpec((tm, d), lambda i, f: (i, 0)),
            pl.BlockSpec((1, d), lambda i, f: (0, 0)),
            pl.BlockSpec((d, tf), lambda i, f: (0, f)),
            pl.BlockSpec((d, tf), lambda i, f: (0, f + nf)),
            pl.BlockSpec((FFN_CONV, tf), lambda i, f: (0, f)),
            pl.BlockSpec((FFN_CONV, tf), lambda i, f: (0, f + nf)),
            pl.BlockSpec((1, tf), lambda i, f: (0, f)),
            pl.BlockSpec((1, tf), lambda i, f: (0, f + nf)),
            pl.BlockSpec((tf, d), lambda i, f: (f, 0)),
        ],
        out_specs=pl.BlockSpec((tm, d), lambda i, f: (i, 0)),
        out_shape=jax.ShapeDtypeStruct((t, d), F32),
        scratch_shapes=[
            pltpu.VMEM((tm, d), BF16),
            pltpu.VMEM((tm + SUBLANES, tf), F32),
            pltpu.VMEM((2, nf, SUBLANES, tf), F32),
        ],
        compiler_params=pltpu.CompilerParams(
            dimension_semantics=("arbitrary", "arbitrary"), vmem_limit_bytes=VMEM_LIMIT),
        name="conv_ffn",
    )(x, g, w_up, w_up, conv_w, conv_w, conv_b, conv_b, w_down)


def _ple_kernel(x_ref, g_ref, wg_ref, p_ref, wp_ref, o_ref, h_ref, *, tn):
    j = pl.program_id(1)

    @pl.when(j == 0)
    def _():
        x = x_ref[...]
        h_ref[...] = (x * _rms_scale(x) * g_ref[...]).astype(BF16)

    gate = jax.nn.sigmoid(_dot(h_ref[...], wg_ref[...]))
    emb = _dot(p_ref[...].astype(BF16), wp_ref[...])
    start = pl.multiple_of(j * tn, tn)
    o_ref[...] = x_ref[:, pl.ds(start, tn)] + gate * emb


def _ple(x, g, w_gate, p, w_proj, *, tm, tn):
    t, d = x.shape
    kern = functools.partial(_ple_kernel, tn=tn)
    return pl.pallas_call(
        kern,
        grid=(t // tm, d // tn),
        in_specs=[
            pl.BlockSpec((tm, d), lambda i, j: (i, 0)),
            pl.BlockSpec((1, d), lambda i, j: (0, 0)),
            pl.BlockSpec((d, tn), lambda i, j: (0, j)),
            pl.BlockSpec((tm, PLE_DIM), lambda i, j: (i, 0)),
            pl.BlockSpec((PLE_DIM, tn), lambda i, j: (0, j)),
        ],
        out_specs=pl.BlockSpec((tm, tn), lambda i, j: (i, j)),
        out_shape=jax.ShapeDtypeStruct((t, d), F32),
        scratch_shapes=[pltpu.VMEM((tm, d), BF16)],
        compiler_params=pltpu.CompilerParams(
            dimension_semantics=("arbitrary", "arbitrary"), vmem_limit_bytes=VMEM_LIMIT),
        name="ple_gate",
    )(x, g, w_gate, p, w_proj)


def _rmsnorm_kernel(x_ref, g_ref, o_ref):
    x = x_ref[...]
    o_ref[...] = x * _rms_scale(x) * g_ref[...]


def _rmsnorm(x, g, *, tm):
    t, d = x.shape
    return pl.pallas_call(
        _rmsnorm_kernel,
        grid=(t // tm,),
        in_specs=[pl.BlockSpec((tm, d), lambda i: (i, 0)),
                  pl.BlockSpec((1, d), lambda i: (0, 0))],
        out_specs=pl.BlockSpec((tm, d), lambda i: (i, 0)),
        out_shape=jax.ShapeDtypeStruct((t, d), F32),
        compiler_params=pltpu.CompilerParams(
            dimension_semantics=("arbitrary",), vmem_limit_bytes=VMEM_LIMIT),
        name="final_rmsnorm",
    )(x, g)


def _reorder_in_proj(w):
    o_x, o_z = 0, SSD_WIDTH
    o_b = o_z + SSD_WIDTH
    o_dt = o_b + 2 * SSD_BC
    o_q = o_dt + SSD_HEADS
    o_gk = o_q + 2 * GLA_QK + 2 * GLA_WIDTH
    pad = jnp.zeros((w.shape[0], LANES - SSD_HEADS - GLA_GATE_RANK), w.dtype)
    cols = [w[:, o_x:o_x + SSD_WIDTH], w[:, o_b:o_dt], w[:, o_z:o_z + SSD_WIDTH],
            w[:, o_q:o_gk], w[:, o_dt:o_q], w[:, o_gk:o_gk + GLA_GATE_RANK], pad]
    return jnp.concatenate(cols, axis=1).astype(BF16)


def _lane_pad(v, offset=0):
    return jnp.zeros((1, LANES), F32).at[0, offset:offset + v.shape[0]].set(v)


def _mixer_constants():
    L = MIX_BLOCK
    r = jnp.arange(L)[:, None]
    c = jnp.arange(L)[None, :]
    tri = (c <= r).astype(BF16)
    same = (r // GLA_CHUNK) == (c // GLA_CHUNK)
    tri64 = ((c <= r) & same).astype(BF16)
    ones64 = same.astype(BF16)
    head = jnp.arange(LANES)[:, None]
    lane_head = jnp.arange(SSD_WIDTH)[None, :] // SSD_HEAD_DIM
    expand = (head == lane_head).astype(BF16)
    return tri, tri64, ones64, expand


def kernel(x, p, norm_mix, w_in, ssd_conv_w, ssd_conv_b, ssd_dt_bias, ssd_a_log, ssd_d, ssd_norm,
           gla_gk_up, gla_gk_bias, gla_norm, w_out, norm_ffn, ffn_w_up, ffn_conv_w, ffn_conv_b,
           ffn_w_down, norm_ple, ple_w_gate, ple_w_proj, norm_final):
    batch, seq, d = x.shape
    depth = w_in.shape[0]
    t = batch * seq
    xf = x.reshape(t, d)
    consts = _mixer_constants()
    row = lambda v: v.reshape(1, -1)

    for i in range(depth):
        proj = _norm_matmul(xf, row(norm_mix[i]), _reorder_in_proj(w_in[i]), tm=1024, tn=1152)
        gk_up_pad = jnp.zeros((LANES, GLA_QK), F32).at[SSD_HEADS:SSD_HEADS + GLA_GATE_RANK].set(gla_gk_up[i])
        gk_up_hi = gk_up_pad.astype(BF16)
        gk_up_lo = (gk_up_pad - gk_up_hi.astype(F32)).astype(BF16)
        mixer_params = (
            ssd_conv_w[i], row(ssd_conv_b[i]), _lane_pad(ssd_dt_bias[i]), _lane_pad(ssd_a_log[i]),
            row(jnp.repeat(ssd_d[i], SSD_HEAD_DIM)), row(ssd_norm[i]),
            gk_up_hi, gk_up_lo, row(gla_gk_bias[i]), row(gla_norm[i]),
        )
        y = _mixer(proj, consts, mixer_params, batch=batch, seq=seq)
        xf = _matmul_residual(y, w_out[i].astype(BF16), xf, tm=1024, tn=1024)
        xf = _ffn(xf, row(norm_ffn[i]), ffn_w_up[i].astype(BF16), ffn_conv_w[i],
                  row(ffn_conv_b[i]), ffn_w_down[i].astype(BF16), tm=512, tf=512, seq=seq)
        xf = _ple(xf, row(norm_ple[i]), ple_w_gate[i].astype(BF16), p[i].reshape(t, PLE_DIM),
                  ple_w_proj[i].astype(BF16), tm=1024, tn=1024)
    out = _rmsnorm(xf, row(norm_final), tm=1024)
    return out.reshape(batch, seq, d)
```

```python
import functools

import jax
import jax.numpy as jnp
from jax import lax
from jax.experimental import pallas as pl
from jax.experimental.pallas import tpu as pltpu

F32 = jnp.float32
BF16 = jnp.bfloat16

NORM_EPS = 1e-6
D_MODEL = 2048
PLE_DIM = 256

SSD_HEADS = 16
SSD_HEAD_DIM = 64
SSD_WIDTH = SSD_HEADS * SSD_HEAD_DIM
SSD_GROUPS = 2
SSD_STATE = 128
SSD_CONV = 4
SSD_BC = SSD_GROUPS * SSD_STATE
SSD_CONV_CH = SSD_WIDTH + 2 * SSD_BC
SSD_GROUP_WIDTH = SSD_WIDTH // SSD_GROUPS
HEADS_PER_GROUP = SSD_HEADS // SSD_GROUPS

GLA_HEADS = 4
GLA_DK = 128
GLA_DV = 256
GLA_QK = GLA_HEADS * GLA_DK
GLA_WIDTH = GLA_HEADS * GLA_DV
GLA_GATE_RANK = 16
GLA_GATE_NORMALIZER = 16.0
GLA_CHUNK = 64

MIX_WIDTH = SSD_WIDTH + GLA_WIDTH
D_FF = 5632
FFN_CONV = 3

LANES = 128
SUBLANES = 8
MXU_COLS = 256
SUB_COLS = 512
UP_ROWS = 512

C_XBC = 0
C_Z = C_XBC + SSD_CONV_CH
C_Q = C_Z + SSD_WIDTH
C_K = C_Q + GLA_QK
C_V = C_K + GLA_QK
C_G = C_V + GLA_WIDTH
C_SMALL = C_G + GLA_WIDTH
PROJ_COLS = C_SMALL + LANES

MIX_BLOCK = 128

VMEM_LIMIT = 56 * 1024 * 1024


def _dot(a, b):
    return jnp.dot(a, b, preferred_element_type=F32)


def _dot_nt(a, b):
    return lax.dot_general(a, b, (((1,), (1,)), ((), ())), preferred_element_type=F32)


def _dot_tn(a, b):
    return lax.dot_general(a, b, (((0,), (0,)), ((), ())), preferred_element_type=F32)


def _split3(a):
    hi = a.astype(BF16)
    r = a - hi.astype(F32)
    mid = r.astype(BF16)
    lo = (r - mid.astype(F32)).astype(BF16)
    return hi, mid, lo


def _dot01_left(m01, a):
    hi, mid, lo = _split3(a)
    return _dot(m01, hi) + _dot(m01, mid) + _dot(m01, lo)


def _dot01_right(a, m01):
    hi, mid, lo = _split3(a)
    return _dot(hi, m01) + _dot(mid, m01) + _dot(lo, m01)


def _silu(x):
    return x * jax.nn.sigmoid(x)


def _softplus(x):
    return jnp.maximum(x, 0.0) + jnp.log1p(jnp.exp(-jnp.abs(x)))


def _rms_scale(x):
    return lax.rsqrt(jnp.mean(x * x, axis=-1, keepdims=True) + NORM_EPS)


def _norm_matmul_kernel(x_ref, g_ref, w_ref, o_ref, h_ref):
    @pl.when(pl.program_id(1) == 0)
    def _():
        x = x_ref[...]
        h_ref[...] = (x * _rms_scale(x) * g_ref[...]).astype(BF16)

    o_ref[...] = _dot(h_ref[...], w_ref[...])


def _norm_matmul(x, g, w, *, tm, tn):
    t, d = x.shape
    n = w.shape[1]
    return pl.pallas_call(
        _norm_matmul_kernel,
        grid=(t // tm, n // tn),
        in_specs=[
            pl.BlockSpec((tm, d), lambda i, j: (i, 0)),
            pl.BlockSpec((1, d), lambda i, j: (0, 0)),
            pl.BlockSpec((d, tn), lambda i, j: (0, j)),
        ],
        out_specs=pl.BlockSpec((tm, tn), lambda i, j: (i, j)),
        out_shape=jax.ShapeDtypeStruct((t, n), F32),
        scratch_shapes=[pltpu.VMEM((tm, d), BF16)],
        compiler_params=pltpu.CompilerParams(
            dimension_semantics=("arbitrary", "arbitrary"), vmem_limit_bytes=VMEM_LIMIT),
        name="norm_in_proj",
    )(x, g, w)


def _matmul_kernel(h_ref, w_ref, o_ref):
    o_ref[...] = _dot(h_ref[...], w_ref[...])


def _matmul(h, w, *, tm, tn):
    t, d = h.shape
    n = w.shape[1]
    return pl.pallas_call(
        _matmul_kernel,
        grid=(t // tm, n // tn),
        in_specs=[
            pl.BlockSpec((tm, d), lambda i, j: (i, 0)),
            pl.BlockSpec((d, tn), lambda i, j: (0, j)),
        ],
        out_specs=pl.BlockSpec((tm, tn), lambda i, j: (i, j)),
        out_shape=jax.ShapeDtypeStruct((t, n), F32),
        compiler_params=pltpu.CompilerParams(
            dimension_semantics=("arbitrary", "arbitrary"), vmem_limit_bytes=VMEM_LIMIT),
        name="in_proj",
    )(h, w)


def _mixer_kernel(proj_ref, convw_ref, convb_ref, dtb_ref, alog_ref, dskip_ref, ssdn_ref,
                  gkup_hi_ref, gkup_lo_ref, gkb_ref, glan_ref,
                  tri_ref, tri64_ref, ones64_ref, expand_ref,
                  y_ref,
                  ext_ref, ssd_state_ref, gla_state_ref):
    L = MIX_BLOCK
    j = pl.program_id(1)

    @pl.when(j == 0)
    def _():
        ext_ref[0:SUBLANES, :] = jnp.zeros((SUBLANES, SSD_CONV_CH), F32)
        ssd_state_ref[...] = jnp.zeros_like(ssd_state_ref)
        gla_state_ref[...] = jnp.zeros_like(gla_state_ref)

    ext_ref[SUBLANES:SUBLANES + L, :] = proj_ref[:, C_XBC:C_XBC + SSD_CONV_CH]
    conv = convb_ref[...]
    for tap in range(SSD_CONV):
        off = SUBLANES - (SSD_CONV - 1 - tap)
        conv = conv + convw_ref[tap:tap + 1, :] * ext_ref[off:off + L, :]
    ext_ref[0:SUBLANES, :] = ext_ref[L:L + SUBLANES, :]
    xbc = _silu(conv)
    xs = xbc[:, 0:SSD_WIDTH]
    bm = xbc[:, SSD_WIDTH:SSD_WIDTH + SSD_BC].astype(BF16)
    cm = xbc[:, SSD_WIDTH + SSD_BC:SSD_CONV_CH].astype(BF16)

    small = proj_ref[:, C_SMALL:C_SMALL + LANES]

    dt = _softplus(small + dtb_ref[...])
    a = dt * (-jnp.exp(alog_ref[...]))
    acs = _dot01_left(tri_ref[...], a)
    a_last = acs[L - 1:L, :]
    expand = expand_ref[...]
    dt_x = _dot01_right(dt, expand)
    e_x = _dot01_right(jnp.exp(acs), expand)
    w_x = _dot01_right(jnp.exp(a_last - acs), expand)
    xdt = xs * dt_x
    xdt_b = xdt.astype(BF16)
    lane = lax.broadcasted_iota(jnp.int32, (L, SSD_WIDTH), 1)
    first_of_pair = (lane & SSD_HEAD_DIM) == 0
    zero_b = jnp.zeros_like(xdt_b)
    xdt_even = jnp.where(first_of_pair, xdt_b, zero_b)
    xdt_odd = jnp.where(first_of_pair, zero_b, xdt_b)
    xdtw_b = (xdt * w_x).astype(BF16)

    acs_t = acs.T
    row = lax.broadcasted_iota(jnp.int32, (L, L), 0)
    col = lax.broadcasted_iota(jnp.int32, (L, L), 1)
    causal = row >= col

    for g in range(SSD_GROUPS):
        gs = slice(g * SSD_STATE, (g + 1) * SSD_STATE)
        gw = slice(g * SSD_GROUP_WIDTH, (g + 1) * SSD_GROUP_WIDTH)
        cm_g = cm[:, gs]
        bm_g = bm[:, gs]
        cb = _dot_nt(cm_g, bm_g)
        pair_out = []
        for pair in range(HEADS_PER_GROUP // 2):
            acc = None
            for sub, rhs_all in ((0, xdt_even), (1, xdt_odd)):
                h = g * HEADS_PER_GROUP + 2 * pair + sub
                diff = acs[:, h:h + 1] - acs_t[h:h + 1, :]
                m = (jnp.where(causal, jnp.exp(diff), 0.0) * cb).astype(BF16)
                c0 = (h // 2) * LANES
                part = _dot(m, rhs_all[:, c0:c0 + LANES])
                acc = part if acc is None else acc + part
            pair_out.append(acc)
        s_prev = ssd_state_ref[g]
        y_off = _dot(cm_g, s_prev.astype(BF16)) * e_x[:, gw]
        new_state = _dot_tn(bm_g, xdtw_b[:, gw])
        ssd_state_ref[g] = e_x[L - 1:L, gw] * s_prev + new_state
        y_g = jnp.concatenate(pair_out, axis=-1) + y_off
        y_g = y_g + xs[:, gw] * dskip_ref[:, gw]
        y_g = y_g * _silu(proj_ref[:, C_Z + g * SSD_GROUP_WIDTH:C_Z + (g + 1) * SSD_GROUP_WIDTH])
        y_g = y_g * _rms_scale(y_g) * ssdn_ref[:, gw]
        y_ref[:, gw] = y_g.astype(y_ref.dtype)

    s_hi = small.astype(BF16)
    s_lo = (small - s_hi.astype(F32)).astype(BF16)
    gkup_hi = gkup_hi_ref[...]
    pre = (_dot(s_hi, gkup_hi) + _dot(s_hi, gkup_lo_ref[...]) + _dot(s_lo, gkup_hi)
           + gkb_ref[...])
    gk = (jnp.minimum(pre, 0.0) - jnp.log1p(jnp.exp(-jnp.abs(pre)))) * (1.0 / GLA_GATE_NORMALIZER)
    bcum = _dot01_left(tri64_ref[...], gk)
    btot = _dot01_left(ones64_ref[...], gk)
    q = proj_ref[:, C_Q:C_Q + GLA_QK] * (GLA_DK ** -0.5)
    k = proj_ref[:, C_K:C_K + GLA_QK]
    q_e = (q * jnp.exp(bcum)).astype(BF16)
    k_e = (k * jnp.exp(-bcum)).astype(BF16)
    k_d = (k * jnp.exp(btot - bcum)).astype(BF16)
    chunk_end_decay = jnp.exp(btot)
    v = proj_ref[:, C_V:C_V + GLA_WIDTH].astype(BF16)
    causal64 = causal & ((row // GLA_CHUNK) == (col // GLA_CHUNK))

    for h in range(GLA_HEADS):
        ks = slice(h * GLA_DK, (h + 1) * GLA_DK)
        vs = slice(h * GLA_DV, (h + 1) * GLA_DV)
        attn = jnp.where(causal64, _dot_nt(q_e[:, ks], k_e[:, ks]), 0.0).astype(BF16)
        o = _dot(attn, v[:, vs])
        state_t = gla_state_ref[h]
        inter = []
        for c in range(L // GLA_CHUNK):
            rs = slice(c * GLA_CHUNK, (c + 1) * GLA_CHUNK)
            inter.append(_dot_nt(q_e[rs, ks], state_t.astype(BF16)))
            kv_t = _dot_tn(v[rs, vs], k_d[rs, ks])
            state_t = chunk_end_decay[c * GLA_CHUNK:c * GLA_CHUNK + 1, ks] * state_t + kv_t
        gla_state_ref[h] = state_t
        o = o + jnp.concatenate(inter, axis=0)
        o = o * _rms_scale(o) * glan_ref[...]
        o = o * _silu(proj_ref[:, C_G + h * GLA_DV:C_G + (h + 1) * GLA_DV])
        y_ref[:, SSD_WIDTH + h * GLA_DV:SSD_WIDTH + (h + 1) * GLA_DV] = o.astype(y_ref.dtype)


def _mixer(proj, consts, params, *, batch, seq):
    L = MIX_BLOCK
    nblk = seq // L
    full = lambda a: pl.BlockSpec(a.shape, lambda b, j: (0,) * a.ndim)
    operands = list(params) + list(consts)
    return pl.pallas_call(
        _mixer_kernel,
        grid=(batch, nblk),
        in_specs=[pl.BlockSpec((L, PROJ_COLS), lambda b, j: (b * nblk + j, 0))]
                 + [full(a) for a in operands],
        out_specs=pl.BlockSpec((L, MIX_WIDTH), lambda b, j: (b * nblk + j, 0)),
        out_shape=jax.ShapeDtypeStruct((batch * seq, MIX_WIDTH), BF16),
        scratch_shapes=[
            pltpu.VMEM((L + SUBLANES, SSD_CONV_CH), F32),
            pltpu.VMEM((SSD_GROUPS, SSD_STATE, SSD_GROUP_WIDTH), F32),
            pltpu.VMEM((GLA_HEADS, GLA_DV, GLA_DK), F32),
        ],
        compiler_params=pltpu.CompilerParams(
            dimension_semantics=("arbitrary", "arbitrary"), vmem_limit_bytes=VMEM_LIMIT),
        name="token_mixer",
    )(proj, *operands)


def _row_update_kernel(*refs, gated, last):
    if gated:
        lhs_ref, w_ref, x_ref, g_ref, p_ref, wp_ref = refs[:6]
        outs = refs[6:]
    else:
        lhs_ref, w_ref, x_ref, g_ref = refs[:4]
        outs = refs[4:]
    if last:
        h_out_ref, x_new_ref = outs
    else:
        x_new_ref, h_out_ref = outs
    tm, d = x_ref.shape
    if gated:
        p_b = p_ref[...].astype(BF16)
    sumsq = jnp.zeros((tm, LANES), F32)
    for c in range(d // SUB_COLS):
        cs = slice(c * SUB_COLS, (c + 1) * SUB_COLS)
        u = _dot(lhs_ref[...], w_ref[:, cs])
        if gated:
            u = jax.nn.sigmoid(u) * _dot(p_b, wp_ref[:, cs])
        xn = x_ref[:, cs] + u
        x_new_ref[:, cs] = xn
        sq = xn * xn
        for s in range(SUB_COLS // LANES):
            sumsq = sumsq + sq[:, s * LANES:(s + 1) * LANES]
    scale = lax.rsqrt(jnp.sum(sumsq, axis=-1, keepdims=True) * (1.0 / d) + NORM_EPS)
    for c in range(d // SUB_COLS):
        cs = slice(c * SUB_COLS, (c + 1) * SUB_COLS)
        h_out_ref[:, cs] = (x_new_ref[:, cs] * scale * g_ref[:, cs]).astype(h_out_ref.dtype)


def _row_update(lhs, w, x, g, p=None, wp=None, *, tm, last=False, name):
    t, d = x.shape
    k = lhs.shape[1]
    gated = p is not None
    row_blk = lambda width: pl.BlockSpec((tm, width), lambda i: (i, 0))
    const = lambda a: pl.BlockSpec(a.shape, lambda i: (0, 0))
    in_specs = [row_blk(k), const(w), row_blk(d), const(g)]
    operands = [lhs, w, x, g]
    if gated:
        in_specs += [row_blk(p.shape[1]), const(wp)]
        operands += [p, wp]
    if last:
        out_specs = row_blk(d)
        out_shape = jax.ShapeDtypeStruct((t, d), F32)
        scratch = [pltpu.VMEM((tm, d), F32)]
    else:
        out_specs = (row_blk(d), row_blk(d))
        out_shape = (jax.ShapeDtypeStruct((t, d), F32), jax.ShapeDtypeStruct((t, d), BF16))
        scratch = []
    return pl.pallas_call(
        functools.partial(_row_update_kernel, gated=gated, last=last),
        grid=(t // tm,),
        in_specs=in_specs,
        out_specs=out_specs,
        out_shape=out_shape,
        scratch_shapes=scratch,
        compiler_params=pltpu.CompilerParams(
            dimension_semantics=("arbitrary",), vmem_limit_bytes=VMEM_LIMIT),
        name=name,
    )(*operands)


def _shift_rows(u3, k):
    rot = pltpu.roll(u3, k, 1)
    prev = jnp.concatenate([jnp.zeros_like(rot[:1]), rot[:-1]], axis=0)
    sub = lax.broadcasted_iota(jnp.int32, (1,) + u3.shape[1:], 1)
    return jnp.where(sub >= k, rot, prev)


def _causal_conv3(u, cw, cb):
    m, n = u.shape
    u3 = u.reshape(m // SUBLANES, SUBLANES, n)
    out = (cb + cw[2:3, :] * u3 + cw[1:2, :] * _shift_rows(u3, 1)
           + cw[0:1, :] * _shift_rows(u3, 2))
    return out.reshape(m, n)


def _ffn_up_kernel(h_ref, wg_ref, wv_ref, cwg_ref, cwv_ref, cbg_ref, cbv_ref, a_ref):
    seq = h_ref.shape[0]
    tf = wg_ref.shape[1]
    half = MXU_COLS
    rows = min(seq, UP_ROWS)
    for c in range(tf // half):
        cs = slice(c * half, (c + 1) * half)
        w = jnp.concatenate([wg_ref[:, cs].astype(BF16), wv_ref[:, cs].astype(BF16)], axis=1)
        cw = jnp.concatenate([cwg_ref[:, cs], cwv_ref[:, cs]], axis=1)
        cb = jnp.concatenate([cbg_ref[:, cs], cbv_ref[:, cs]], axis=1)
        u = jnp.concatenate(
            [_dot(h_ref[r * rows:(r + 1) * rows, :], w) for r in range(seq // rows)], axis=0)
        conv = _causal_conv3(u, cw, cb)
        a_ref[:, cs] = (_silu(conv[:, 0:half]) * conv[:, half:2 * half]).astype(a_ref.dtype)


def _ffn_up(h, w_up, conv_w, conv_b, *, seq, tf):
    t, d = h.shape
    nf = D_FF // tf
    return pl.pallas_call(
        _ffn_up_kernel,
        grid=(t // seq, nf),
        in_specs=[
            pl.BlockSpec((seq, d), lambda i, f: (i, 0)),
            pl.BlockSpec((d, tf), lambda i, f: (0, f)),
            pl.BlockSpec((d, tf), lambda i, f: (0, f + nf)),
            pl.BlockSpec((FFN_CONV, tf), lambda i, f: (0, f)),
            pl.BlockSpec((FFN_CONV, tf), lambda i, f: (0, f + nf)),
            pl.BlockSpec((1, tf), lambda i, f: (0, f)),
            pl.BlockSpec((1, tf), lambda i, f: (0, f + nf)),
        ],
        out_specs=pl.BlockSpec((seq, tf), lambda i, f: (i, f)),
        out_shape=jax.ShapeDtypeStruct((t, D_FF), BF16),
        compiler_params=pltpu.CompilerParams(
            dimension_semantics=("arbitrary", "arbitrary"), vmem_limit_bytes=VMEM_LIMIT),
        name="ffn_up_conv_gate",
    )(h, w_up, w_up, conv_w, conv_w, conv_b, conv_b)


def _ffn_down_kernel(a_ref, wd_ref, x_ref, g_ref, x_new_ref, h_out_ref):
    k = pl.program_id(1)
    d = x_ref.shape[1]

    @pl.when(k == 0)
    def _():
        x_new_ref[...] = x_ref[...]

    for c in range(d // SUB_COLS):
        cs = slice(c * SUB_COLS, (c + 1) * SUB_COLS)
        x_new_ref[:, cs] += _dot(a_ref[...], wd_ref[:, cs])

    @pl.when(k == pl.num_programs(1) - 1)
    def _():
        xn = x_new_ref[...]
        h_out_ref[...] = (xn * _rms_scale(xn) * g_ref[...]).astype(h_out_ref.dtype)


def _ffn_down(act, w_down, x, g, *, tm, tk):
    t, d = x.shape
    kk = act.shape[1]
    return pl.pallas_call(
        _ffn_down_kernel,
        grid=(t // tm, kk // tk),
        in_specs=[
            pl.BlockSpec((tm, tk), lambda i, k: (i, k)),
            pl.BlockSpec((tk, d), lambda i, k: (k, 0)),
            pl.BlockSpec((tm, d), lambda i, k: (i, 0)),
            pl.BlockSpec((1, d), lambda i, k: (0, 0)),
        ],
        out_specs=(pl.BlockSpec((tm, d), lambda i, k: (i, 0)),
                   pl.BlockSpec((tm, d), lambda i, k: (i, 0))),
        out_shape=(jax.ShapeDtypeStruct((t, d), F32), jax.ShapeDtypeStruct((t, d), BF16)),
        compiler_params=pltpu.CompilerParams(
            dimension_semantics=("arbitrary", "arbitrary"), vmem_limit_bytes=VMEM_LIMIT),
        name="ffn_down_residual",
    )(act, w_down, x, g)


def _reorder_in_proj(w):
    o_x, o_z = 0, SSD_WIDTH
    o_b = o_z + SSD_WIDTH
    o_dt = o_b + 2 * SSD_BC
    o_q = o_dt + SSD_HEADS
    o_gk = o_q + 2 * GLA_QK + 2 * GLA_WIDTH
    pad = jnp.zeros((w.shape[0], LANES - SSD_HEADS - GLA_GATE_RANK), w.dtype)
    cols = [w[:, o_x:o_x + SSD_WIDTH], w[:, o_b:o_dt], w[:, o_z:o_z + SSD_WIDTH],
            w[:, o_q:o_gk], w[:, o_dt:o_q], w[:, o_gk:o_gk + GLA_GATE_RANK], pad]
    return jnp.concatenate(cols, axis=1).astype(BF16)


def _lane_pad(v, offset=0):
    return jnp.zeros((1, LANES), F32).at[0, offset:offset + v.shape[0]].set(v)


def _mixer_constants():
    L = MIX_BLOCK
    r = jnp.arange(L)[:, None]
    c = jnp.arange(L)[None, :]
    tri = (c <= r).astype(BF16)
    same = (r // GLA_CHUNK) == (c // GLA_CHUNK)
    tri64 = ((c <= r) & same).astype(BF16)
    ones64 = same.astype(BF16)
    head = jnp.arange(LANES)[:, None]
    lane_head = jnp.arange(SSD_WIDTH)[None, :] // SSD_HEAD_DIM
    expand = (head == lane_head).astype(BF16)
    return tri, tri64, ones64, expand


def kernel(x, p, norm_mix, w_in, ssd_conv_w, ssd_conv_b, ssd_dt_bias, ssd_a_log, ssd_d, ssd_norm,
           gla_gk_up, gla_gk_bias, gla_norm, w_out, norm_ffn, ffn_w_up, ffn_conv_w, ffn_conv_b,
           ffn_w_down, norm_ple, ple_w_gate, ple_w_proj, norm_final):
    batch, seq, d = x.shape
    depth = w_in.shape[0]
    t = batch * seq
    xf = x.reshape(t, d)
    consts = _mixer_constants()
    row = lambda v: v.reshape(1, -1)

    h = None
    out = None
    for i in range(depth):
        w_in_b = _reorder_in_proj(w_in[i])
        if i == 0:
            proj = _norm_matmul(xf, row(norm_mix[i]), w_in_b, tm=1024, tn=1152)
        else:
            proj = _matmul(h, w_in_b, tm=1024, tn=1152)
        gk_up_pad = jnp.zeros((LANES, GLA_QK), F32).at[SSD_HEADS:SSD_HEADS + GLA_GATE_RANK].set(gla_gk_up[i])
        gk_up_hi = gk_up_pad.astype(BF16)
        gk_up_lo = (gk_up_pad - gk_up_hi.astype(F32)).astype(BF16)
        mixer_params = (
            ssd_conv_w[i], row(ssd_conv_b[i]), _lane_pad(ssd_dt_bias[i]), _lane_pad(ssd_a_log[i]),
            row(jnp.repeat(ssd_d[i], SSD_HEAD_DIM)), row(ssd_norm[i]),
            gk_up_hi, gk_up_lo, row(gla_gk_bias[i]), row(gla_norm[i]),
        )
        y = _mixer(proj, consts, mixer_params, batch=batch, seq=seq)
        xf, h = _row_update(y, w_out[i].astype(BF16), xf, row(norm_ffn[i]), tm=512,
                            name="out_proj_residual")
        act = _ffn_up(h, ffn_w_up[i], ffn_conv_w[i], row(ffn_conv_b[i]), seq=seq, tf=512)
        xf, h = _ffn_down(act, ffn_w_down[i].astype(BF16), xf, row(norm_ple[i]), tm=1024, tk=512)
        last = i == depth - 1
        g_next = norm_final if last else norm_mix[i + 1]
        res = _row_update(h, ple_w_gate[i].astype(BF16), xf, row(g_next),
                          p[i].reshape(t, PLE_DIM), ple_w_proj[i].astype(BF16),
                          tm=512, last=last, name="ple_gate")
        if last:
            out = res
        else:
            xf, h = res
    return out.reshape(batch, seq, d)
```

```python
import functools

import jax
import jax.numpy as jnp
from jax import lax
from jax.experimental import pallas as pl
from jax.experimental.pallas import tpu as pltpu

F32 = jnp.float32
BF16 = jnp.bfloat16

NORM_EPS = 1e-6
D_MODEL = 2048
PLE_DIM = 256

SSD_HEADS = 16
SSD_HEAD_DIM = 64
SSD_WIDTH = SSD_HEADS * SSD_HEAD_DIM
SSD_GROUPS = 2
SSD_STATE = 128
SSD_CONV = 4
SSD_BC = SSD_GROUPS * SSD_STATE
SSD_CONV_CH = SSD_WIDTH + 2 * SSD_BC
SSD_GROUP_WIDTH = SSD_WIDTH // SSD_GROUPS
HEADS_PER_GROUP = SSD_HEADS // SSD_GROUPS

GLA_HEADS = 4
GLA_DK = 128
GLA_DV = 256
GLA_QK = GLA_HEADS * GLA_DK
GLA_WIDTH = GLA_HEADS * GLA_DV
GLA_GATE_RANK = 16
GLA_GATE_NORMALIZER = 16.0
GLA_CHUNK = 64

MIX_WIDTH = SSD_WIDTH + GLA_WIDTH
D_FF = 5632
FFN_CONV = 3

LANES = 128
SUBLANES = 8
MXU_COLS = 256
SUB_COLS = 512
UP_ROWS = 512

C_XBC = 0
C_Z = C_XBC + SSD_CONV_CH
C_Q = C_Z + SSD_WIDTH
C_K = C_Q + GLA_QK
C_V = C_K + GLA_QK
C_G = C_V + GLA_WIDTH
C_SMALL = C_G + GLA_WIDTH
PROJ_COLS = C_SMALL + LANES

MIX_BLOCK = 128

VMEM_LIMIT = 56 * 1024 * 1024


def _dot(a, b):
    return jnp.dot(a, b, preferred_element_type=F32)


def _dot_nt(a, b):
    return lax.dot_general(a, b, (((1,), (1,)), ((), ())), preferred_element_type=F32)


def _dot_tn(a, b):
    return lax.dot_general(a, b, (((0,), (0,)), ((), ())), preferred_element_type=F32)


def _split3(a):
    hi = a.astype(BF16)
    r = a - hi.astype(F32)
    mid = r.astype(BF16)
    lo = (r - mid.astype(F32)).astype(BF16)
    return hi, mid, lo


def _dot01_left(m01, a):
    hi, mid, lo = _split3(a)
    return _dot(m01, hi) + _dot(m01, mid) + _dot(m01, lo)


def _dot01_right(a, m01):
    hi, mid, lo = _split3(a)
    return _dot(hi, m01) + _dot(mid, m01) + _dot(lo, m01)


def _silu(x):
    return x * jax.nn.sigmoid(x)


def _softplus(x):
    return jnp.maximum(x, 0.0) + jnp.log1p(jnp.exp(-jnp.abs(x)))


def _rms_scale(x):
    return lax.rsqrt(jnp.mean(x * x, axis=-1, keepdims=True) + NORM_EPS)


def _resident(stacked, layer):
    return pl.BlockSpec((None,) + stacked.shape[1:], lambda *_: (layer, 0, 0),
                        pipeline_mode=pl.Buffered(1))


def _in_proj_kernel(*refs, normed):
    if normed:
        h_ref, w_ref, o_ref = refs
        h = h_ref[...]
    else:
        x_ref, g_ref, w_ref, o_ref = refs
        x = x_ref[...]
        h = (x * _rms_scale(x) * g_ref[...]).astype(BF16)
    o_ref[...] = _dot(h, w_ref[...])


def _in_proj(rows, w, layer, g=None, *, tm):
    t, d = rows.shape
    n = w.shape[2]
    normed = g is None
    in_specs = [pl.BlockSpec((tm, d), lambda i: (i, 0))]
    operands = [rows]
    if not normed:
        in_specs.append(pl.BlockSpec((1, d), lambda i: (0, 0)))
        operands.append(g)
    in_specs.append(_resident(w, layer))
    operands.append(w)
    return pl.pallas_call(
        functools.partial(_in_proj_kernel, normed=normed),
        grid=(t // tm,),
        in_specs=in_specs,
        out_specs=pl.BlockSpec((tm, n), lambda i: (i, 0)),
        out_shape=jax.ShapeDtypeStruct((t, n), F32),
        compiler_params=pltpu.CompilerParams(
            dimension_semantics=("arbitrary",), vmem_limit_bytes=VMEM_LIMIT),
        name="in_proj" if normed else "norm_in_proj",
    )(*operands)


def _mixer_kernel(proj_ref, convw_ref, convb_ref, dtb_ref, alog_ref, dskip_ref, ssdn_ref,
                  gkup_hi_ref, gkup_lo_ref, gkb_ref, glan_ref,
                  tri_ref, tri64_ref, ones64_ref, expand_ref,
                  y_ref,
                  ext_ref, ssd_state_ref, gla_state_ref):
    L = MIX_BLOCK
    j = pl.program_id(1)

    @pl.when(j == 0)
    def _():
        ext_ref[0:SUBLANES, :] = jnp.zeros((SUBLANES, SSD_CONV_CH), F32)
        ssd_state_ref[...] = jnp.zeros_like(ssd_state_ref)
        gla_state_ref[...] = jnp.zeros_like(gla_state_ref)

    ext_ref[SUBLANES:SUBLANES + L, :] = proj_ref[:, C_XBC:C_XBC + SSD_CONV_CH]
    conv = convb_ref[...]
    for tap in range(SSD_CONV):
        off = SUBLANES - (SSD_CONV - 1 - tap)
        conv = conv + convw_ref[tap:tap + 1, :] * ext_ref[off:off + L, :]
    ext_ref[0:SUBLANES, :] = ext_ref[L:L + SUBLANES, :]
    xbc = _silu(conv)
    xs = xbc[:, 0:SSD_WIDTH]
    bm = xbc[:, SSD_WIDTH:SSD_WIDTH + SSD_BC].astype(BF16)
    cm = xbc[:, SSD_WIDTH + SSD_BC:SSD_CONV_CH].astype(BF16)

    small = proj_ref[:, C_SMALL:C_SMALL + LANES]

    dt = _softplus(small + dtb_ref[...])
    a = dt * (-jnp.exp(alog_ref[...]))
    acs = _dot01_left(tri_ref[...], a)
    a_last = acs[L - 1:L, :]
    expand = expand_ref[...]
    dt_x = _dot01_right(dt, expand)
    e_x = _dot01_right(jnp.exp(acs), expand)
    w_x = _dot01_right(jnp.exp(a_last - acs), expand)
    xdt = xs * dt_x
    xdt_b = xdt.astype(BF16)
    lane = lax.broadcasted_iota(jnp.int32, (L, SSD_WIDTH), 1)
    first_of_pair = (lane & SSD_HEAD_DIM) == 0
    zero_b = jnp.zeros_like(xdt_b)
    xdt_even = jnp.where(first_of_pair, xdt_b, zero_b)
    xdt_odd = jnp.where(first_of_pair, zero_b, xdt_b)
    xdtw_b = (xdt * w_x).astype(BF16)

    acs_t = acs.T
    row = lax.broadcasted_iota(jnp.int32, (L, L), 0)
    col = lax.broadcasted_iota(jnp.int32, (L, L), 1)
    causal = row >= col

    for g in range(SSD_GROUPS):
        gs = slice(g * SSD_STATE, (g + 1) * SSD_STATE)
        gw = slice(g * SSD_GROUP_WIDTH, (g + 1) * SSD_GROUP_WIDTH)
        cm_g = cm[:, gs]
        bm_g = bm[:, gs]
        cb = _dot_nt(cm_g, bm_g)
        pair_out = []
        for pair in range(HEADS_PER_GROUP // 2):
            acc = None
            for sub, rhs_all in ((0, xdt_even), (1, xdt_odd)):
                h = g * HEADS_PER_GROUP + 2 * pair + sub
                diff = acs[:, h:h + 1] - acs_t[h:h + 1, :]
                m = (jnp.where(causal, jnp.exp(diff), 0.0) * cb).astype(BF16)
                c0 = (h // 2) * LANES
                part = _dot(m, rhs_all[:, c0:c0 + LANES])
                acc = part if acc is None else acc + part
            pair_out.append(acc)
        s_prev = ssd_state_ref[g]
        y_off = _dot(cm_g, s_prev.astype(BF16)) * e_x[:, gw]
        new_state = _dot_tn(bm_g, xdtw_b[:, gw])
        ssd_state_ref[g] = e_x[L - 1:L, gw] * s_prev + new_state
        y_g = jnp.concatenate(pair_out, axis=-1) + y_off
        y_g = y_g + xs[:, gw] * dskip_ref[:, gw]
        y_g = y_g * _silu(proj_ref[:, C_Z + g * SSD_GROUP_WIDTH:C_Z + (g + 1) * SSD_GROUP_WIDTH])
        y_g = y_g * _rms_scale(y_g) * ssdn_ref[:, gw]
        y_ref[:, gw] = y_g.astype(y_ref.dtype)

    s_hi = small.astype(BF16)
    s_lo = (small - s_hi.astype(F32)).astype(BF16)
    gkup_hi = gkup_hi_ref[...]
    pre = (_dot(s_hi, gkup_hi) + _dot(s_hi, gkup_lo_ref[...]) + _dot(s_lo, gkup_hi)
           + gkb_ref[...])
    gk = (jnp.minimum(pre, 0.0) - jnp.log1p(jnp.exp(-jnp.abs(pre)))) * (1.0 / GLA_GATE_NORMALIZER)
    bcum = _dot01_left(tri64_ref[...], gk)
    btot = _dot01_left(ones64_ref[...], gk)
    q = proj_ref[:, C_Q:C_Q + GLA_QK] * (GLA_DK ** -0.5)
    k = proj_ref[:, C_K:C_K + GLA_QK]
    q_e = (q * jnp.exp(bcum)).astype(BF16)
    k_e = (k * jnp.exp(-bcum)).astype(BF16)
    k_d = (k * jnp.exp(btot - bcum)).astype(BF16)
    chunk_end_decay = jnp.exp(btot)
    v = proj_ref[:, C_V:C_V + GLA_WIDTH].astype(BF16)
    causal64 = causal & ((row // GLA_CHUNK) == (col // GLA_CHUNK))

    for h in range(GLA_HEADS):
        ks = slice(h * GLA_DK, (h + 1) * GLA_DK)
        vs = slice(h * GLA_DV, (h + 1) * GLA_DV)
        attn = jnp.where(causal64, _dot_nt(q_e[:, ks], k_e[:, ks]), 0.0).astype(BF16)
        o = _dot(attn, v[:, vs])
        state_t = gla_state_ref[h]
        inter = []
        for c in range(L // GLA_CHUNK):
            rs = slice(c * GLA_CHUNK, (c + 1) * GLA_CHUNK)
            inter.append(_dot_nt(q_e[rs, ks], state_t.astype(BF16)))
            kv_t = _dot_tn(v[rs, vs], k_d[rs, ks])
            state_t = chunk_end_decay[c * GLA_CHUNK:c * GLA_CHUNK + 1, ks] * state_t + kv_t
        gla_state_ref[h] = state_t
        o = o + jnp.concatenate(inter, axis=0)
        o = o * _rms_scale(o) * glan_ref[...]
        o = o * _silu(proj_ref[:, C_G + h * GLA_DV:C_G + (h + 1) * GLA_DV])
        y_ref[:, SSD_WIDTH + h * GLA_DV:SSD_WIDTH + (h + 1) * GLA_DV] = o.astype(y_ref.dtype)


def _mixer(proj, consts, params, *, batch, seq):
    L = MIX_BLOCK
    nblk = seq // L
    full = lambda a: pl.BlockSpec(a.shape, lambda b, j: (0,) * a.ndim)
    operands = list(params) + list(consts)
    return pl.pallas_call(
        _mixer_kernel,
        grid=(batch, nblk),
        in_specs=[pl.BlockSpec((L, PROJ_COLS), lambda b, j: (b * nblk + j, 0))]
                 + [full(a) for a in operands],
        out_specs=pl.BlockSpec((L, MIX_WIDTH), lambda b, j: (b * nblk + j, 0)),
        out_shape=jax.ShapeDtypeStruct((batch * seq, MIX_WIDTH), BF16),
        scratch_shapes=[
            pltpu.VMEM((L + SUBLANES, SSD_CONV_CH), F32),
            pltpu.VMEM((SSD_GROUPS, SSD_STATE, SSD_GROUP_WIDTH), F32),
            pltpu.VMEM((GLA_HEADS, GLA_DV, GLA_DK), F32),
        ],
        compiler_params=pltpu.CompilerParams(
            dimension_semantics=("arbitrary", "arbitrary"), vmem_limit_bytes=VMEM_LIMIT),
        name="token_mixer",
    )(proj, *operands)


def _row_update_kernel(*refs, gated, last):
    if gated:
        lhs_ref, w_ref, x_ref, g_ref, p_ref, wp_ref = refs[:6]
        outs = refs[6:]
    else:
        lhs_ref, w_ref, x_ref, g_ref = refs[:4]
        outs = refs[4:]
    if last:
        h_out_ref, x_new_ref = outs
    else:
        x_new_ref, h_out_ref = outs
    tm, d = x_ref.shape
    if gated:
        p_b = p_ref[...].astype(BF16)
    sumsq = jnp.zeros((tm, LANES), F32)
    for c in range(d // SUB_COLS):
        cs = slice(c * SUB_COLS, (c + 1) * SUB_COLS)
        u = _dot(lhs_ref[...], w_ref[:, cs])
        if gated:
            u = jax.nn.sigmoid(u) * _dot(p_b, wp_ref[:, cs])
        xn = x_ref[:, cs] + u
        x_new_ref[:, cs] = xn
        sq = xn * xn
        for s in range(SUB_COLS // LANES):
            sumsq = sumsq + sq[:, s * LANES:(s + 1) * LANES]
    scale = lax.rsqrt(jnp.sum(sumsq, axis=-1, keepdims=True) * (1.0 / d) + NORM_EPS)
    for c in range(d // SUB_COLS):
        cs = slice(c * SUB_COLS, (c + 1) * SUB_COLS)
        h_out_ref[:, cs] = (x_new_ref[:, cs] * scale * g_ref[:, cs]).astype(h_out_ref.dtype)


def _row_update(lhs, w, x, g, layer, p=None, wp=None, *, tm, last=False, name):
    t, d = x.shape
    k = lhs.shape[1]
    gated = p is not None
    row_blk = lambda width: pl.BlockSpec((tm, width), lambda i: (i, 0))
    in_specs = [row_blk(k), _resident(w, layer), row_blk(d), pl.BlockSpec((1, d), lambda i: (0, 0))]
    operands = [lhs, w, x, g]
    if gated:
        in_specs += [pl.BlockSpec((None, tm, p.shape[2]), lambda i: (layer, i, 0)),
                     _resident(wp, layer)]
        operands += [p, wp]
    if last:
        out_specs = row_blk(d)
        out_shape = jax.ShapeDtypeStruct((t, d), F32)
        scratch = [pltpu.VMEM((tm, d), F32)]
    else:
        out_specs = (row_blk(d), row_blk(d))
        out_shape = (jax.ShapeDtypeStruct((t, d), F32), jax.ShapeDtypeStruct((t, d), BF16))
        scratch = []
    return pl.pallas_call(
        functools.partial(_row_update_kernel, gated=gated, last=last),
        grid=(t // tm,),
        in_specs=in_specs,
        out_specs=out_specs,
        out_shape=out_shape,
        scratch_shapes=scratch,
        compiler_params=pltpu.CompilerParams(
            dimension_semantics=("arbitrary",), vmem_limit_bytes=VMEM_LIMIT),
        name=name,
    )(*operands)


def _shift_rows(u3, k):
    rot = pltpu.roll(u3, k, 1)
    prev = jnp.concatenate([jnp.zeros_like(rot[:1]), rot[:-1]], axis=0)
    sub = lax.broadcasted_iota(jnp.int32, (1,) + u3.shape[1:], 1)
    return jnp.where(sub >= k, rot, prev)


def _causal_conv3(u, cw, cb):
    m, n = u.shape
    u3 = u.reshape(m // SUBLANES, SUBLANES, n)
    out = (cb + cw[2:3, :] * u3 + cw[1:2, :] * _shift_rows(u3, 1)
           + cw[0:1, :] * _shift_rows(u3, 2))
    return out.reshape(m, n)


def _ffn_up_kernel(h_ref, wg_ref, wv_ref, cwg_ref, cwv_ref, cbg_ref, cbv_ref, a_ref):
    seq = h_ref.shape[0]
    tf = wg_ref.shape[1]
    half = MXU_COLS
    rows = min(seq, UP_ROWS)
    for c in range(tf // half):
        cs = slice(c * half, (c + 1) * half)
        w = jnp.concatenate([wg_ref[:, cs].astype(BF16), wv_ref[:, cs].astype(BF16)], axis=1)
        cw = jnp.concatenate([cwg_ref[:, cs], cwv_ref[:, cs]], axis=1)
        cb = jnp.concatenate([cbg_ref[:, cs], cbv_ref[:, cs]], axis=1)
        u = jnp.concatenate(
            [_dot(h_ref[r * rows:(r + 1) * rows, :], w) for r in range(seq // rows)], axis=0)
        conv = _causal_conv3(u, cw, cb)
        a_ref[:, cs] = (_silu(conv[:, 0:half]) * conv[:, half:2 * half]).astype(a_ref.dtype)


def _ffn_up(h, w_up, conv_w, conv_b, layer, *, seq, tf):
    t, d = h.shape
    nf = D_FF // tf

    def stacked(rows, half):
        return pl.BlockSpec((None, rows, tf), lambda i, f: (layer, 0, f + half * nf))

    return pl.pallas_call(
        _ffn_up_kernel,
        grid=(t // seq, nf),
        in_specs=[
            pl.BlockSpec((seq, d), lambda i, f: (i, 0)),
            stacked(d, 0), stacked(d, 1),
            stacked(FFN_CONV, 0), stacked(FFN_CONV, 1),
            stacked(1, 0), stacked(1, 1),
        ],
        out_specs=pl.BlockSpec((seq, tf), lambda i, f: (i, f)),
        out_shape=jax.ShapeDtypeStruct((t, D_FF), BF16),
        compiler_params=pltpu.CompilerParams(
            dimension_semantics=("arbitrary", "arbitrary"), vmem_limit_bytes=VMEM_LIMIT),
        name="ffn_up_conv_gate",
    )(h, w_up, w_up, conv_w, conv_w, conv_b, conv_b)


def _ffn_down_kernel(a_ref, wd_ref, x_ref, g_ref, x_new_ref, h_out_ref):
    tm, d = x_ref.shape
    sumsq = jnp.zeros((tm, LANES), F32)
    for c in range(d // SUB_COLS):
        cs = slice(c * SUB_COLS, (c + 1) * SUB_COLS)
        xn = x_ref[:, cs] + _dot(a_ref[...], wd_ref[:, cs])
        x_new_ref[:, cs] = xn
        sq = xn * xn
        for s in range(SUB_COLS // LANES):
            sumsq = sumsq + sq[:, s * LANES:(s + 1) * LANES]
    scale = lax.rsqrt(jnp.sum(sumsq, axis=-1, keepdims=True) * (1.0 / d) + NORM_EPS)
    for c in range(d // SUB_COLS):
        cs = slice(c * SUB_COLS, (c + 1) * SUB_COLS)
        h_out_ref[:, cs] = (x_new_ref[:, cs] * scale * g_ref[:, cs]).astype(h_out_ref.dtype)


def _ffn_down(act, w_down, x, g, layer, *, tm):
    t, d = x.shape
    kk = act.shape[1]
    row_blk = lambda width: pl.BlockSpec((tm, width), lambda i: (i, 0))
    return pl.pallas_call(
        _ffn_down_kernel,
        grid=(t // tm,),
        in_specs=[row_blk(kk), _resident(w_down, layer), row_blk(d),
                  pl.BlockSpec((1, d), lambda i: (0, 0))],
        out_specs=(row_blk(d), row_blk(d)),
        out_shape=(jax.ShapeDtypeStruct((t, d), F32), jax.ShapeDtypeStruct((t, d), BF16)),
        compiler_params=pltpu.CompilerParams(
            dimension_semantics=("arbitrary",), vmem_limit_bytes=VMEM_LIMIT),
        name="ffn_down_residual",
    )(act, w_down, x, g)


def _reorder_in_proj(w):
    o_x, o_z = 0, SSD_WIDTH
    o_b = o_z + SSD_WIDTH
    o_dt = o_b + 2 * SSD_BC
    o_q = o_dt + SSD_HEADS
    o_gk = o_q + 2 * GLA_QK + 2 * GLA_WIDTH
    pad = jnp.zeros(w.shape[:2] + (LANES - SSD_HEADS - GLA_GATE_RANK,), BF16)
    cols = [w[..., o_x:o_x + SSD_WIDTH], w[..., o_b:o_dt], w[..., o_z:o_z + SSD_WIDTH],
            w[..., o_q:o_gk], w[..., o_dt:o_q], w[..., o_gk:o_gk + GLA_GATE_RANK]]
    return jnp.concatenate([c.astype(BF16) for c in cols] + [pad], axis=-1)


def _lane_pad(v, offset=0):
    return jnp.zeros((1, LANES), F32).at[0, offset:offset + v.shape[0]].set(v)


def _mixer_constants():
    L = MIX_BLOCK
    r = jnp.arange(L)[:, None]
    c = jnp.arange(L)[None, :]
    tri = (c <= r).astype(BF16)
    same = (r // GLA_CHUNK) == (c // GLA_CHUNK)
    tri64 = ((c <= r) & same).astype(BF16)
    ones64 = same.astype(BF16)
    head = jnp.arange(LANES)[:, None]
    lane_head = jnp.arange(SSD_WIDTH)[None, :] // SSD_HEAD_DIM
    expand = (head == lane_head).astype(BF16)
    return tri, tri64, ones64, expand


def kernel(x, p, norm_mix, w_in, ssd_conv_w, ssd_conv_b, ssd_dt_bias, ssd_a_log, ssd_d, ssd_norm,
           gla_gk_up, gla_gk_bias, gla_norm, w_out, norm_ffn, ffn_w_up, ffn_conv_w, ffn_conv_b,
           ffn_w_down, norm_ple, ple_w_gate, ple_w_proj, norm_final):
    batch, seq, d = x.shape
    depth = w_in.shape[0]
    t = batch * seq
    xf = x.reshape(t, d)
    consts = _mixer_constants()
    row = lambda v: v.reshape(1, -1)

    p_rows = p.reshape(depth, t, PLE_DIM)
    conv_b_rows = ffn_conv_b.reshape(depth, 1, 2 * D_FF)
    w_in_b = _reorder_in_proj(w_in)
    w_out_b = w_out.astype(BF16)
    w_down_b = ffn_w_down.astype(BF16)
    w_gate_b = ple_w_gate.astype(BF16)
    w_pproj_b = ple_w_proj.astype(BF16)
    h = None
    out = None
    for i in range(depth):
        if i == 0:
            proj = _in_proj(xf, w_in_b, i, row(norm_mix[i]), tm=256)
        else:
            proj = _in_proj(h, w_in_b, i, tm=512)
        gk_up_pad = jnp.zeros((LANES, GLA_QK), F32).at[SSD_HEADS:SSD_HEADS + GLA_GATE_RANK].set(gla_gk_up[i])
        gk_up_hi = gk_up_pad.astype(BF16)
        gk_up_lo = (gk_up_pad - gk_up_hi.astype(F32)).astype(BF16)
        mixer_params = (
            ssd_conv_w[i], row(ssd_conv_b[i]), _lane_pad(ssd_dt_bias[i]), _lane_pad(ssd_a_log[i]),
            row(jnp.repeat(ssd_d[i], SSD_HEAD_DIM)), row(ssd_norm[i]),
            gk_up_hi, gk_up_lo, row(gla_gk_bias[i]), row(gla_norm[i]),
        )
        y = _mixer(proj, consts, mixer_params, batch=batch, seq=seq)
        xf, h = _row_update(y, w_out_b, xf, row(norm_ffn[i]), i, tm=512,
                            name="out_proj_residual")
        act = _ffn_up(h, ffn_w_up, ffn_conv_w, conv_b_rows, i, seq=seq, tf=512)
        xf, h = _ffn_down(act, w_down_b, xf, row(norm_ple[i]), i, tm=512)
        last = i == depth - 1
        g_next = norm_final if last else norm_mix[i + 1]
        res = _row_update(h, w_gate_b, xf, row(g_next), i, p_rows, w_pproj_b,
                          tm=512, last=last, name="ple_gate")
        if last:
            out = res
        else:
            xf, h = res
    return out.reshape(batch, seq, d)
```
